```python
import jax, jax.numpy as jnp
from jax import lax
import numpy as np

D_MODEL = 2048
BATCH = 8
SEQ = 4096
DEPTH = 1

CHUNK = 64
Q_BLOCK = 128
FOX_HEADS = 8
FOX_HEAD_DIM = 128
FOX_WIDTH = FOX_HEADS * FOX_HEAD_DIM
CONV_WIDTH = 1024
CONV_K = 3
N_BRANCHES = 2
PLE_DIM = 256
N_GROUPS = 4
EXPERTS_PER_GROUP = 8
N_EXPERTS = N_GROUPS * EXPERTS_PER_GROUP
TOP_K_IN_GROUP = 2
EXPERT_FF = 512
NORM_EPS = 1e-6
IN_COLS = 3 * FOX_WIDTH + FOX_HEADS + 3 * CONV_WIDTH + N_BRANCHES * D_MODEL

kernel_name = "hybrid_fox_shortconv_hiermoe_block"


def rmsnorm(x, g):
    xf = x.astype(jnp.float32)
    y = xf * lax.rsqrt(jnp.mean(xf * xf, axis=-1, keepdims=True) + NORM_EPS)
    return (y * g.astype(jnp.float32)).astype(x.dtype)


def split_points():
    sizes = [FOX_WIDTH, FOX_WIDTH, FOX_WIDTH, FOX_HEADS, CONV_WIDTH, CONV_WIDTH, CONV_WIDTH]
    return tuple(int(v) for v in np.cumsum(sizes))


def fox_attention(q, k, v, log_f):
    b, s, h, dh = q.shape
    nb = s // Q_BLOCK
    c = jnp.cumsum(log_f, axis=1).transpose(0, 2, 1)
    scale = dh ** -0.5
    kpos = jnp.arange(s)
    q_blocks = q.reshape(b, nb, Q_BLOCK, h, dh).transpose(1, 0, 2, 3, 4)
    c_blocks = c.reshape(b, h, nb, Q_BLOCK).transpose(2, 0, 1, 3)
    starts = jnp.arange(nb) * Q_BLOCK

    def block(args):
        qb, cb, start = args
        logits = jnp.einsum('bqhd,bkhd->bhqk', qb, k).astype(jnp.float32) * scale
        logits = logits + cb[..., :, None] - c[:, :, None, :]
        qpos = start + jnp.arange(Q_BLOCK)
        mask = kpos[None, :] <= qpos[:, None]
        logits = jnp.where(mask, logits, -jnp.inf)
        probs = jax.nn.softmax(logits, axis=-1).astype(v.dtype)
        return jnp.einsum('bhqk,bkhd->bqhd', probs, v)

    out = lax.map(block, (q_blocks, c_blocks, starts))
    return out.transpose(1, 0, 2, 3, 4).reshape(b, s, h * dh)


def causal_short_conv(z, w):
    s = z.shape[1]
    zp = jnp.pad(z, ((0, 0), (CONV_K - 1, 0), (0, 0)))
    return sum(w[j] * zp[:, j:j + s] for j in range(CONV_K))


def hier_moe(hn, w_rg, b_rg, w_re, b_re, w_gu, w_down):
    t = hn.shape[0]
    g_logits = (hn @ w_rg).astype(jnp.float32) + b_rg
    g_probs = jax.nn.softmax(g_logits, axis=-1)
    grp = jnp.argmax(g_logits, axis=-1).astype(jnp.int32)
    p_grp = jnp.take_along_axis(g_probs, grp[:, None], axis=1)
    e_logits = ((hn @ w_re).astype(jnp.float32) + b_re).reshape(t, N_GROUPS, EXPERTS_PER_GROUP)
    e_sel = jnp.take_along_axis(e_logits, grp[:, None, None], axis=1)[:, 0]
    e_probs = jax.nn.softmax(e_sel, axis=-1)
    vals, idx = lax.top_k(e_probs, TOP_K_IN_GROUP)
    wts = p_grp * vals / jnp.sum(vals, axis=-1, keepdims=True)
    eid = grp[:, None] * EXPERTS_PER_GROUP + idx.astype(jnp.int32)

    flat_e = eid.reshape(-1)
    order = jnp.argsort(flat_e)
    tok = order // TOP_K_IN_GROUP
    group_sizes = jnp.bincount(flat_e, length=N_EXPERTS).astype(jnp.int32)
    xs = hn[tok]
    gu = lax.ragged_dot(xs, w_gu, group_sizes)
    gate, up = jnp.split(gu, 2, axis=-1)
    ys = lax.ragged_dot(jax.nn.silu(gate) * up, w_down, group_sizes)
    ys = ys * wts.reshape(-1)[order][:, None].astype(ys.dtype)
    return jax.ops.segment_sum(ys, tok, num_segments=t)


def setup_inputs(seed: int = 0) -> dict:
    key = jax.random.key(seed)
    ks = jax.random.split(key, 20)
    f32 = jnp.float32
    D, L = D_MODEL, DEPTH
    nrm = lambda k, shape, fan_in: jax.random.normal(k, shape, f32) * (fan_in ** -0.5)
    gain = lambda k, shape: 1.0 + 0.01 * jax.random.normal(k, shape, f32)
    return {
        "x": jax.random.normal(ks[0], (BATCH, SEQ, D), f32),
        "p": jax.random.normal(ks[1], (DEPTH, BATCH, SEQ, PLE_DIM), f32),
        "g_mix": gain(ks[2], (L, D)),
        "w_in": nrm(ks[3], (L, D, IN_COLS), D),
        "b_f": jax.random.uniform(ks[4], (L, FOX_HEADS), f32, 1.0, 4.0),
        "conv_w": nrm(ks[5], (L, CONV_K, CONV_WIDTH), CONV_K),
        "w_branch_a": nrm(ks[6], (L, FOX_WIDTH, D), FOX_WIDTH),
        "w_branch_b": nrm(ks[7], (L, CONV_WIDTH, D), CONV_WIDTH),
        "w_out": nrm(ks[8], (L, D, D), D),
        "g_ffn": gain(ks[9], (L, D)),
        "w_router_group": nrm(ks[10], (L, D, N_GROUPS), D),
        "b_router_group": 0.01 * jax.random.normal(ks[11], (L, N_GROUPS), f32),
        "w_router_expert": nrm(ks[12], (L, D, N_EXPERTS), D),
        "b_router_expert": 0.01 * jax.random.normal(ks[13], (L, N_EXPERTS), f32),
        "w_gate_up": nrm(ks[14], (L, N_EXPERTS, D, 2 * EXPERT_FF), D),
        "w_down": nrm(ks[15], (L, N_EXPERTS, EXPERT_FF, D), EXPERT_FF),
        "g_ple": gain(ks[16], (L, D)),
        "w_ple_gate": nrm(ks[17], (L, D, D), D),
        "w_ple_proj": nrm(ks[18], (L, PLE_DIM, D), PLE_DIM),
        "g_final": gain(ks[19], (D,)),
    }


def reference(x, p, g_mix, w_in, b_f, conv_w, w_branch_a, w_branch_b, w_out, g_ffn,
              w_router_group, b_router_group, w_router_expert, b_router_expert,
              w_gate_up, w_down, g_ple, w_ple_gate, w_ple_proj, g_final):
    b, s, d = x.shape
    cuts = split_points()
    for i in range(DEPTH):
        h = rmsnorm(x, g_mix[i])
        proj = h @ w_in[i]
        q, k, v, f_logit, u, bg, cg, gate_logit = jnp.split(proj, cuts, axis=-1)
        log_f = jax.nn.log_sigmoid(f_logit.astype(jnp.float32) + b_f[i])
        shp = (b, s, FOX_HEADS, FOX_HEAD_DIM)
        attn = fox_attention(q.reshape(shp), k.reshape(shp), v.reshape(shp), log_f)
        conv = bg * causal_short_conv(cg * u, conv_w[i])
        ga, gb = jnp.split(jax.nn.sigmoid(gate_logit), N_BRANCHES, axis=-1)
        merged = ga * (attn @ w_branch_a[i]) + gb * (conv @ w_branch_b[i])
        x = x + merged @ w_out[i]
        hn = rmsnorm(x, g_ffn[i]).reshape(b * s, d)
        x = x + hier_moe(hn, w_router_group[i], b_router_group[i], w_router_expert[i],
                         b_router_expert[i], w_gate_up[i], w_down[i]).reshape(b, s, d)
        hp = rmsnorm(x, g_ple[i])
        x = x + jax.nn.sigmoid(hp @ w_ple_gate[i]) * (p[i] @ w_ple_proj[i])
    return rmsnorm(x, g_final)
```

```python
import functools

import jax
import jax.numpy as jnp
from jax import lax
from jax.experimental import pallas as pl
from jax.experimental.pallas import tpu as pltpu

F32 = jnp.float32
BF16 = jnp.bfloat16
I32 = jnp.int32

LANES = 128
SUBLANES = 8
VMEM_LIMIT = 56 * 1024 * 1024

HEADS = 8
HEAD_DIM = 128
FOX_WIDTH = HEADS * HEAD_DIM
CONV_WIDTH = 1024
CONV_K = 3
N_GROUPS = 4
EXPERTS_PER_GROUP = 8
N_EXPERTS = N_GROUPS * EXPERTS_PER_GROUP
TOP_K = 2
EXPERT_FF = 512
NORM_EPS = 1e-6

PROJ_COLS = 3 * FOX_WIDTH + 3 * CONV_WIDTH

TM_PROJ = 512
TN_PROJ = 2048
TQ = 512
TM_MIX = 256
TM_DISPATCH = 512
TM_EXPERT = 256
TM_COMBINE = 256
ROUTER_FIRST_EXPERT_LANE = N_GROUPS


def _rms_scale(x):
    return lax.rsqrt(jnp.mean(x * x, axis=-1, keepdims=True) + NORM_EPS)


def _split3_bf16(v):
    hi = v.astype(BF16)
    r1 = v - hi.astype(F32)
    mid = r1.astype(BF16)
    lo = (r1 - mid.astype(F32)).astype(BF16)
    return hi, mid, lo


def _inproj_kernel(x_ref, g_ref, w_ref, cscale_ref, wf_ref, bf_ref, o_ref, c_ref,
                   h_scr, carry_scr, *, tiles_per_seq):
    i = pl.program_id(0)
    j = pl.program_id(1)
    tm = x_ref.shape[0]

    @pl.when(j == 0)
    def _():
        x = x_ref[...]
        hb = (x * _rms_scale(x) * g_ref[...]).astype(BF16)
        h_scr[...] = hb
        f = jnp.dot(hb, wf_ref[...], preferred_element_type=F32) + bf_ref[...]
        logf = jnp.minimum(f, 0.0) - jnp.log1p(jnp.exp(-jnp.abs(f)))
        hi, mid, lo = _split3_bf16(logf)
        row = lax.broadcasted_iota(I32, (tm, tm), 0)
        col = lax.broadcasted_iota(I32, (tm, tm), 1)
        tri = jnp.where(col <= row, 1.0, 0.0).astype(BF16)
        cs = (jnp.dot(tri, hi, preferred_element_type=F32)
              + jnp.dot(tri, mid, preferred_element_type=F32)
              + jnp.dot(tri, lo, preferred_element_type=F32))
        carry = jnp.where(i % tiles_per_seq == 0, 0.0, carry_scr[...])
        c = cs + carry
        c_ref[...] = c
        carry_scr[...] = c[tm - 1:tm, :]

    acc = jnp.dot(h_scr[...], w_ref[...], preferred_element_type=F32)
    o_ref[...] = (acc * cscale_ref[...]).astype(o_ref.dtype)


def _inproj(x2, g, w_main, cscale, w_f, b_f, seq):
    t, d = x2.shape
    n = w_main.shape[1]
    grid = (t // TM_PROJ, n // TN_PROJ)
    return pl.pallas_call(
        functools.partial(_inproj_kernel, tiles_per_seq=seq // TM_PROJ),
        grid=grid,
        in_specs=[
            pl.BlockSpec((TM_PROJ, d), lambda i, j: (i, 0)),
            pl.BlockSpec((1, d), lambda i, j: (0, 0)),
            pl.BlockSpec((d, TN_PROJ), lambda i, j: (0, j)),
            pl.BlockSpec((1, TN_PROJ), lambda i, j: (0, j)),
            pl.BlockSpec((d, LANES), lambda i, j: (0, 0)),
            pl.BlockSpec((1, LANES), lambda i, j: (0, 0)),
        ],
        out_specs=[
            pl.BlockSpec((TM_PROJ, TN_PROJ), lambda i, j: (i, j)),
            pl.BlockSpec((TM_PROJ, LANES), lambda i, j: (i, 0)),
        ],
        out_shape=[
            jax.ShapeDtypeStruct((t, n), BF16),
            jax.ShapeDtypeStruct((t, LANES), F32),
        ],
        scratch_shapes=[
            pltpu.VMEM((TM_PROJ, d), BF16),
            pltpu.VMEM((1, LANES), F32),
        ],
        compiler_params=pltpu.CompilerParams(
            dimension_semantics=("arbitrary", "arbitrary"),
            vmem_limit_bytes=VMEM_LIMIT),
        name="inproj",
    )(x2, g, w_main, cscale, w_f, b_f)


def _attn_kernel(q_ref, k_ref, v_ref, cq_ref, ck_ref, o_ref, m_scr, l_scr, acc_scr):
    i = pl.program_id(1)
    j = pl.program_id(2)
    tq = q_ref.shape[1]
    tk = k_ref.shape[1]

    @pl.when(j == 0)
    def _():
        m_scr[...] = jnp.full(m_scr.shape, -jnp.inf, F32)
        l_scr[...] = jnp.zeros(l_scr.shape, F32)
        acc_scr[...] = jnp.zeros(acc_scr.shape, F32)

    def step(masked):
        if masked:
            row = lax.broadcasted_iota(I32, (tq, tk), 0)
            col = lax.broadcasted_iota(I32, (tq, tk), 1)
            keep = col <= row
        for h in range(HEADS):
            hs = slice(h * HEAD_DIM, (h + 1) * HEAD_DIM)
            q = q_ref[0, :, hs]
            k = k_ref[0, :, hs]
            v = v_ref[0, :, hs]
            s = lax.dot_general(q, k, (((1,), (1,)), ((), ())),
                                preferred_element_type=F32)
            s = s + cq_ref[0, :, h:h + 1] - ck_ref[0, h:h + 1, :]
            if masked:
                s = jnp.where(keep, s, -jnp.inf)
            m_prev = m_scr[h]
            m_new = jnp.maximum(m_prev, jnp.max(s, axis=1, keepdims=True))
            alpha = jnp.exp(m_prev - m_new)
            p = jnp.exp(s - m_new)
            l_scr[h] = alpha * l_scr[h] + jnp.sum(p, axis=1, keepdims=True)
            acc_scr[:, hs] = alpha * acc_scr[:, hs] + jnp.dot(
                p.astype(BF16), v, preferred_element_type=F32)
            m_scr[h] = m_new

    @pl.when(j < i)
    def _():
        step(False)

    @pl.when(j == i)
    def _():
        step(True)
        for h in range(HEADS):
            hs = slice(h * HEAD_DIM, (h + 1) * HEAD_DIM)
            o_ref[0, :, hs] = (acc_scr[:, hs] / l_scr[h]).astype(o_ref.dtype)


def _attention(proj3, c_col, c_row):
    b, s, _ = proj3.shape
    nq = s // TQ
    kv_idx = lambda bb, i, j: jnp.minimum(i, j)
    return pl.pallas_call(
        _attn_kernel,
        grid=(b, nq, nq),
        in_specs=[
            pl.BlockSpec((1, TQ, FOX_WIDTH), lambda bb, i, j: (bb, i, 0)),
            pl.BlockSpec((1, TQ, FOX_WIDTH), lambda bb, i, j: (bb, kv_idx(bb, i, j), 1)),
            pl.BlockSpec((1, TQ, FOX_WIDTH), lambda bb, i, j: (bb, kv_idx(bb, i, j), 2)),
            pl.BlockSpec((1, TQ, LANES), lambda bb, i, j: (bb, i, 0)),
            pl.BlockSpec((1, HEADS, TQ), lambda bb, i, j: (bb, 0, kv_idx(bb, i, j))),
        ],
        out_specs=pl.BlockSpec((1, TQ, FOX_WIDTH), lambda bb, i, j: (bb, i, 0)),
        out_shape=jax.ShapeDtypeStruct((b, s, FOX_WIDTH), BF16),
        scratch_shapes=[
            pltpu.VMEM((HEADS, TQ, 1), F32),
            pltpu.VMEM((HEADS, TQ, 1), F32),
            pltpu.VMEM((TQ, FOX_WIDTH), F32),
        ],
        compiler_params=pltpu.CompilerParams(
            dimension_semantics=("arbitrary", "arbitrary", "arbitrary"),
            vmem_limit_bytes=VMEM_LIMIT),
        name="fox_attention",
    )(proj3, proj3, proj3, c_col, c_row)


def _mix_kernel(attn_ref, u_ref, bg_ref, cg_ref, uh_ref, cgh_ref, ga_ref, gb_ref, x_ref,
                convw_ref, wa_ref, wb_ref, wo_ref, gffn_ref, wr_ref, br_ref,
                x1_ref, ri_ref, rw_ref, cnt_ref, cnt_scr, *, tiles_per_seq):
    i = pl.program_id(0)
    tm = x_ref.shape[0]

    z = cg_ref[...].astype(F32) * u_ref[...].astype(F32)
    halo_on = jnp.where(i % tiles_per_seq == 0, 0.0, 1.0)
    zh = cgh_ref[...].astype(F32) * uh_ref[...].astype(F32) * halo_on
    rowi = lax.broadcasted_iota(I32, (tm, 1), 0)
    z1 = pltpu.roll(z, 1, 0)
    z1 = jnp.where(rowi == 0, zh[SUBLANES - 1:SUBLANES, :], z1)
    z2 = pltpu.roll(z, 2, 0)
    z2 = jnp.where(rowi == 0, zh[SUBLANES - 2:SUBLANES - 1, :], z2)
    z2 = jnp.where(rowi == 1, zh[SUBLANES - 1:SUBLANES, :], z2)
    cw = convw_ref[...]
    conv = cw[0:1, :] * z2 + cw[1:2, :] * z1 + cw[2:3, :] * z
    bm = (bg_ref[...].astype(F32) * conv).astype(BF16)

    a = jnp.dot(attn_ref[...], wa_ref[...], preferred_element_type=F32)
    bv = jnp.dot(bm, wb_ref[...], preferred_element_type=F32)
    merged = (jax.nn.sigmoid(ga_ref[...].astype(F32)) * a
              + jax.nn.sigmoid(gb_ref[...].astype(F32)) * bv)
    x1 = x_ref[...] + jnp.dot(merged.astype(BF16), wo_ref[...], preferred_element_type=F32)
    x1_ref[...] = x1

    hn = x1 * _rms_scale(x1) * gffn_ref[...]
    hn_hi = hn.astype(BF16)
    hn_lo = (hn - hn_hi.astype(F32)).astype(BF16)
    hh = jnp.dot(hn_hi, wr_ref[...], preferred_element_type=F32)
    lh = jnp.dot(hn_lo, wr_ref[:, :LANES], preferred_element_type=F32)
    lg = hh[:, :LANES] + hh[:, LANES:] + lh + br_ref[...]

    lane = lax.broadcasted_iota(I32, (tm, LANES), 1)
    lanef = lane.astype(F32)
    neg = -jnp.inf
    big = float(LANES)
    gmask = lane < N_GROUPS
    gl = jnp.where(gmask, lg, neg)
    gmax = jnp.max(gl, axis=1, keepdims=True)
    grp = jnp.min(jnp.where(gl == gmax, lanef, big), axis=1, keepdims=True)
    p_grp = 1.0 / jnp.sum(jnp.where(gmask, jnp.exp(gl - gmax), 0.0), axis=1, keepdims=True)
    first = ROUTER_FIRST_EXPERT_LANE
    lane_grp = ((lane - first) >> 3).astype(F32)
    emask = lane_grp == grp
    el = jnp.where(emask, lg, neg)
    m1 = jnp.max(el, axis=1, keepdims=True)
    i1 = jnp.min(jnp.where(el == m1, lanef, big), axis=1, keepdims=True)
    oh1 = lanef == i1
    el2 = jnp.where(oh1, neg, el)
    m2 = jnp.max(el2, axis=1, keepdims=True)
    i2 = jnp.min(jnp.where(el2 == m2, lanef, big), axis=1, keepdims=True)
    oh2 = lanef == i2
    e2 = jnp.exp(m2 - m1)
    w1 = p_grp / (1.0 + e2)
    w2 = p_grp * e2 / (1.0 + e2)

    @pl.when(i == 0)
    def _():
        cnt_scr[...] = jnp.zeros(cnt_scr.shape, F32)

    ohs = jnp.where(oh1 | oh2, 1.0, 0.0)
    row = lax.broadcasted_iota(I32, (tm, tm), 0)
    col = lax.broadcasted_iota(I32, (tm, tm), 1)
    tri = jnp.where(col < row, 1.0, 0.0).astype(BF16)
    before = jnp.dot(tri, ohs.astype(BF16), preferred_element_type=F32) + cnt_scr[...]
    r1 = jnp.sum(jnp.where(oh1, before, 0.0), axis=1, keepdims=True)
    r2 = jnp.sum(jnp.where(oh2, before, 0.0), axis=1, keepdims=True)
    cnt_new = before[tm - 1:tm, :] + ohs[tm - 1:tm, :]
    cnt_scr[...] = cnt_new
    cnt_ref[...] = cnt_new

    e1 = i1 - float(first)
    e2i = i2 - float(first)
    ri = jnp.where(lane == 0, e1, jnp.where(lane == 1, e2i,
                   jnp.where(lane == 2, r1, jnp.where(lane == 3, r2, 0.0))))
    ri_ref[...] = ri.astype(I32)
    rw_ref[...] = jnp.where(lane == 0, w1, jnp.where(lane == 1, w2, 0.0))


def _mix(attn2, proj2, x2, conv_w, w_a, w_b, w_o, g_ffn, w_r, b_r, seq):
    t, d = x2.shape
    tm = TM_MIX
    cw = CONV_WIDTH
    halo_blk = tm // SUBLANES
    halo_idx = lambda i: jnp.maximum(i * halo_blk - 1, 0)
    const = lambda shape: pl.BlockSpec(shape, lambda i: (0,) * len(shape),
                                       pipeline_mode=pl.Buffered(1))
    gate_col0 = PROJ_COLS // d
    return pl.pallas_call(
        functools.partial(_mix_kernel, tiles_per_seq=seq // tm),
        grid=(t // tm,),
        in_specs=[
            pl.BlockSpec((tm, FOX_WIDTH), lambda i: (i, 0)),
            pl.BlockSpec((tm, cw), lambda i: (i, 3)),
            pl.BlockSpec((tm, cw), lambda i: (i, 4)),
            pl.BlockSpec((tm, cw), lambda i: (i, 5)),
            pl.BlockSpec((SUBLANES, cw), lambda i: (halo_idx(i), 3)),
            pl.BlockSpec((SUBLANES, cw), lambda i: (halo_idx(i), 5)),
            pl.BlockSpec((tm, d), lambda i: (i, gate_col0)),
            pl.BlockSpec((tm, d), lambda i: (i, gate_col0 + 1)),
            pl.BlockSpec((tm, d), lambda i: (i, 0)),
            const((SUBLANES, cw)),
            const((FOX_WIDTH, d)),
            const((cw, d)),
            const((d, d)),
            const((1, d)),
            const((d, 2 * LANES)),
            const((1, LANES)),
        ],
        out_specs=[
            pl.BlockSpec((tm, d), lambda i: (i, 0)),
            pl.BlockSpec((tm, LANES), lambda i: (i, 0)),
            pl.BlockSpec((tm, LANES), lambda i: (i, 0)),
            pl.BlockSpec((1, LANES), lambda i: (0, 0)),
        ],
        out_shape=[
            jax.ShapeDtypeStruct((t, d), F32),
            jax.ShapeDtypeStruct((t, LANES), I32),
            jax.ShapeDtypeStruct((t, LANES), F32),
            jax.ShapeDtypeStruct((1, LANES), F32),
        ],
        scratch_shapes=[pltpu.VMEM((1, LANES), F32)],
        compiler_params=pltpu.CompilerParams(
            dimension_semantics=("arbitrary",),
            vmem_limit_bytes=VMEM_LIMIT),
        name="mix_router",
    )(attn2, proj2, proj2, proj2, proj2, proj2, proj2, proj2, x2,
      conv_w, w_a, w_b, w_o, g_ffn, w_r, b_r)


def _dispatch_kernel(pad_start_ref, pad_cnt_ref, x1_ref, g_ref, pos_hbm, xs_hbm,
                     hn_scr, zero_scr, pos_smem, sem_pos, sem_rows, sem_pad):
    i = pl.program_id(0)
    tm = x1_ref.shape[0]

    pos_cp = pltpu.make_async_copy(pos_hbm.at[pl.ds(i, 1)], pos_smem, sem_pos)
    pos_cp.start()

    def pad_copy(e, r):
        return pltpu.make_async_copy(zero_scr.at[pl.ds(0, 1)],
                                     xs_hbm.at[pl.ds(pad_start_ref[e] + r, 1)], sem_pad)

    @pl.when(i == 0)
    def _():
        zero_scr[...] = jnp.zeros(zero_scr.shape, F32)
        for wait in (False, True):
            def per_expert(e, carry, wait=wait):
                def per_row(r, c):
                    cp = pad_copy(e, r)
                    if wait:
                        cp.wait()
                    else:
                        cp.start()
                    return c
                return lax.fori_loop(0, pad_cnt_ref[e], per_row, carry)
            lax.fori_loop(0, N_EXPERTS, per_expert, 0)

    x1 = x1_ref[...]
    hn_scr[...] = x1 * _rms_scale(x1) * g_ref[...]
    pos_cp.wait()

    def row_copy(r, k):
        return pltpu.make_async_copy(hn_scr.at[pl.ds(r, 1)],
                                     xs_hbm.at[pl.ds(pos_smem[0, TOP_K * r + k], 1)], sem_rows)

    def issue(r, c):
        for k in range(TOP_K):
            row_copy(r, k).start()
        return c

    def drain(r, c):
        for k in range(TOP_K):
            row_copy(r, k).wait()
        return c

    lax.fori_loop(0, tm, issue, 0)
    lax.fori_loop(0, tm, drain, 0)


def _dispatch(pad_start, pad_cnt, x1, g_ffn, pos_tiles, n_rows):
    t, d = x1.shape
    tm = TM_DISPATCH
    return pl.pallas_call(
        _dispatch_kernel,
        grid_spec=pltpu.PrefetchScalarGridSpec(
            num_scalar_prefetch=2,
            grid=(t // tm,),
            in_specs=[
                pl.BlockSpec((tm, d), lambda i, ps, pc: (i, 0)),
                pl.BlockSpec((1, d), lambda i, ps, pc: (0, 0)),
                pl.BlockSpec(memory_space=pl.ANY),
            ],
            out_specs=pl.BlockSpec(memory_space=pl.ANY),
            scratch_shapes=[
                pltpu.VMEM((tm, d), F32),
                pltpu.VMEM((SUBLANES, d), F32),
                pltpu.SMEM((1, TOP_K * tm), I32),
                pltpu.SemaphoreType.DMA,
                pltpu.SemaphoreType.DMA,
                pltpu.SemaphoreType.DMA,
            ],
        ),
        out_shape=jax.ShapeDtypeStruct((n_rows, d), F32),
        compiler_params=pltpu.CompilerParams(
            dimension_semantics=("arbitrary",),
            vmem_limit_bytes=VMEM_LIMIT),
        name="moe_dispatch",
    )(pad_start, pad_cnt, x1, g_ffn, pos_tiles)


def _experts_kernel(tile_expert_ref, n_tiles_ref, xs_ref, wgu_ref, wd_ref, ys_ref):
    j = pl.program_id(0)

    @pl.when(j < n_tiles_ref[0])
    def _():
        x = xs_ref[...].astype(BF16)
        gu = jnp.dot(x, wgu_ref[0], preferred_element_type=F32)
        gate = gu[:, :EXPERT_FF]
        up = gu[:, EXPERT_FF:]
        act = (gate * jax.nn.sigmoid(gate) * up).astype(BF16)
        ys_ref[...] = jnp.dot(act, wd_ref[0], preferred_element_type=F32)


def _experts(tile_expert, n_tiles, xs, w_gu, w_down):
    n_rows, d = xs.shape
    tm = TM_EXPERT
    row_idx = lambda j, te, nt: (jnp.minimum(j, nt[0] - 1), 0)
    return pl.pallas_call(
        _experts_kernel,
        grid_spec=pltpu.PrefetchScalarGridSpec(
            num_scalar_prefetch=2,
            grid=(n_rows // tm,),
            in_specs=[
                pl.BlockSpec((tm, d), row_idx),
                pl.BlockSpec((1, d, 2 * EXPERT_FF), lambda j, te, nt: (te[j], 0, 0)),
                pl.BlockSpec((1, EXPERT_FF, d), lambda j, te, nt: (te[j], 0, 0)),
            ],
            out_specs=pl.BlockSpec((tm, d), row_idx),
        ),
        out_shape=jax.ShapeDtypeStruct((n_rows, d), F32),
        compiler_params=pltpu.CompilerParams(
            dimension_semantics=("arbitrary",),
            vmem_limit_bytes=VMEM_LIMIT),
        name="moe_experts",
    )(tile_expert, n_tiles, xs, w_gu, w_down)


def _combine_kernel(x1_ref, rw_ref, p_ref, gple_ref, wpg_ref, wpe_ref, gfin_ref,
                    pos_hbm, ys_hbm, o_ref, ybuf, pos_smem, sem_pos, sem_rows,
                    *, apply_final_norm):
    i = pl.program_id(0)
    n = pl.num_programs(0)
    tm = x1_ref.shape[0]
    slot = i % 2
    nslot = (i + 1) % 2

    def pos_copy(tile, s):
        return pltpu.make_async_copy(pos_hbm.at[pl.ds(tile, 1)], pos_smem.at[s], sem_pos.at[s])

    def row_copy(s, r, k, src_row):
        return pltpu.make_async_copy(ys_hbm.at[pl.ds(src_row, 1)],
                                     ybuf.at[s, k, pl.ds(r, 1)], sem_rows.at[s])

    def issue_rows(s):
        def body(r, c):
            for k in range(TOP_K):
                row_copy(s, r, k, pos_smem[s, 0, TOP_K * r + k]).start()
            return c
        lax.fori_loop(0, tm, body, 0)

    @pl.when(i == 0)
    def _():
        first = pos_copy(0, 0)
        first.start()
        first.wait()
        issue_rows(0)
        pos_copy(1, 1).start()

    @pl.when(i + 1 < n)
    def _():
        pos_copy(i + 1, nslot).wait()
        issue_rows(nslot)

    @pl.when(i + 2 < n)
    def _():
        pos_copy(i + 2, slot).start()

    def drain(r, c):
        for k in range(TOP_K):
            row_copy(slot, r, k, 0).wait()
        return c
    lax.fori_loop(0, tm, drain, 0)

    rw = rw_ref[...]
    x2 = (x1_ref[...] + rw[:, 0:1] * ybuf[slot, 0] + rw[:, 1:2] * ybuf[slot, 1])
    hp = (x2 * _rms_scale(x2) * gple_ref[...]).astype(BF16)
    gate = jax.nn.sigmoid(jnp.dot(hp, wpg_ref[...], preferred_element_type=F32))
    emb = jnp.dot(p_ref[...].astype(BF16), wpe_ref[...], preferred_element_type=F32)
    x3 = x2 + gate * emb
    if apply_final_norm:
        x3 = x3 * _rms_scale(x3) * gfin_ref[...]
    o_ref[...] = x3


def _combine(x1, route_w, p2, g_ple, w_pg, w_pe, g_final, pos_tiles, ys, apply_final_norm):
    t, d = x1.shape
    tm = TM_COMBINE
    pdim = p2.shape[1]
    const = lambda shape: pl.BlockSpec(shape, lambda i: (0,) * len(shape),
                                       pipeline_mode=pl.Buffered(1))
    return pl.pallas_call(
        functools.partial(_combine_kernel, apply_final_norm=apply_final_norm),
        grid=(t // tm,),
        in_specs=[
            pl.BlockSpec((tm, d), lambda i: (i, 0)),
            pl.BlockSpec((tm, LANES), lambda i: (i, 0)),
            pl.BlockSpec((tm, pdim), lambda i: (i, 0)),
            const((1, d)),
            const((d, d)),
            const((pdim, d)),
            const((1, d)),
            pl.BlockSpec(memory_space=pl.ANY),
            pl.BlockSpec(memory_space=pl.ANY),
        ],
        out_specs=pl.BlockSpec((tm, d), lambda i: (i, 0)),
        out_shape=jax.ShapeDtypeStruct((t, d), F32),
        scratch_shapes=[
            pltpu.VMEM((2, TOP_K, tm, d), F32),
            pltpu.SMEM((2, 1, TOP_K * tm), I32),
            pltpu.SemaphoreType.DMA((2,)),
            pltpu.SemaphoreType.DMA((2,)),
        ],
        compiler_params=pltpu.CompilerParams(
            dimension_semantics=("arbitrary",),
            vmem_limit_bytes=VMEM_LIMIT),
        name="combine_ple",
    )(x1, route_w, p2, g_ple, w_pg, w_pe, g_final, pos_tiles, ys)


def _layer(x2, p2, batch, seq, g_mix, w_in, b_f, conv_w, w_a, w_b, w_o, g_ffn, w_rg, b_rg,
           w_re, b_re, w_gu, w_down, g_ple, w_pg, w_pe, g_final, apply_final_norm):
    t, d = x2.shape
    fw = FOX_WIDTH
    f0 = 3 * fw
    u0 = f0 + HEADS
    gt0 = u0 + 3 * CONV_WIDTH

    w_main = jnp.concatenate([w_in[:, :f0], w_in[:, u0:gt0], w_in[:, gt0:]], axis=1).astype(BF16)
    w_f = jnp.pad(w_in[:, f0:u0], ((0, 0), (0, LANES - HEADS))).astype(BF16)
    b_fp = jnp.pad(b_f, (0, LANES - HEADS)).reshape(1, LANES)
    cscale = jnp.concatenate([jnp.full((fw,), HEAD_DIM ** -0.5, F32),
                              jnp.ones((w_main.shape[1] - fw,), F32)]).reshape(1, -1)
    w_r = jnp.pad(jnp.concatenate([w_rg, w_re], axis=1),
                  ((0, 0), (0, LANES - N_GROUPS - N_EXPERTS)))
    w_r_hi = w_r.astype(BF16)
    w_r_lo = (w_r - w_r_hi.astype(F32)).astype(BF16)
    w_r2 = jnp.concatenate([w_r_hi, w_r_lo], axis=1)
    b_r = jnp.pad(jnp.concatenate([b_rg, b_re]), (0, LANES - N_GROUPS - N_EXPERTS)).reshape(1, LANES)
    conv_wp = jnp.pad(conv_w, ((0, SUBLANES - CONV_K), (0, 0)))

    proj, c = _inproj(x2, g_mix.reshape(1, d), w_main, cscale, w_f, b_fp, seq)
    c_col = c.reshape(batch, seq, LANES)
    c_row = jnp.transpose(c_col[:, :, :HEADS], (0, 2, 1))
    attn = _attention(proj.reshape(batch, seq, -1), c_col, c_row)

    x1, route_i, route_w, counts = _mix(
        attn.reshape(t, fw), proj, x2, conv_wp, w_a.astype(BF16), w_b.astype(BF16),
        w_o.astype(BF16), g_ffn.reshape(1, d), w_r2, b_r, seq)

    tme = TM_EXPERT
    n_rows = t * TOP_K + N_EXPERTS * tme
    cnt = counts[0, N_GROUPS:N_GROUPS + N_EXPERTS].astype(I32)
    padded = ((cnt + tme - 1) // tme) * tme
    ends = jnp.cumsum(padded)
    offs = ends - padded
    pos = offs[route_i[:, :TOP_K]] + route_i[:, TOP_K:2 * TOP_K]
    tile_ends = ends // tme
    tile_expert = jnp.minimum(
        jnp.searchsorted(tile_ends, jnp.arange(n_rows // tme, dtype=I32), side="right"),
        N_EXPERTS - 1).astype(I32)
    n_tiles = tile_ends[-1:].astype(I32)

    xs = _dispatch(offs + cnt, padded - cnt, x1, g_ffn.reshape(1, d),
                   pos.reshape(t // TM_DISPATCH, TOP_K * TM_DISPATCH), n_rows)
    ys = _experts(tile_expert, n_tiles, xs, w_gu.astype(BF16), w_down.astype(BF16))
    return _combine(x1, route_w, p2, g_ple.reshape(1, d), w_pg.astype(BF16), w_pe.astype(BF16),
                    g_final.reshape(1, d), pos.reshape(t // TM_COMBINE, TOP_K * TM_COMBINE),
                    ys, apply_final_norm)


def kernel(x, p, g_mix, w_in, b_f, conv_w, w_branch_a, w_branch_b, w_out, g_ffn,
           w_router_group, b_router_group, w_router_expert, b_router_expert,
           w_gate_up, w_down, g_ple, w_ple_gate, w_ple_proj, g_final):
    batch, seq, d = x.shape
    depth = p.shape[0]
    x2 = x.reshape(batch * seq, d)
    for i in range(depth):
        x2 = _layer(x2, p[i].reshape(batch * seq, -1), batch, seq, g_mix[i], w_in[i], b_f[i],
                    conv_w[i], w_branch_a[i], w_branch_b[i], w_out[i], g_ffn[i],
                    w_router_group[i], b_router_group[i], w_router_expert[i],
                    b_router_expert[i], w_gate_up[i], w_down[i], g_ple[i], w_ple_gate[i],
                    w_ple_proj[i], g_final, apply_final_norm=(i == depth - 1))
    return x2.reshape(batch, seq, d)
```
